```python
import jax, jax.numpy as jnp
from jax import lax
import numpy as np

D_MODEL = 1024
BATCH = 8
SEQ = 4096
DEPTH = 1

D_MIX = D_MODEL
D_GMLP = D_MIX // 2
GMLP_GROUPS = 4
GMLP_GROUP_DIM = D_GMLP // GMLP_GROUPS
CHUNK = 128
D_SB = D_MIX - D_GMLP
SB_HEADS = 8
SB_HEAD_DIM = D_SB // SB_HEADS
Q_BLOCK = 128
D_IN = 2 * D_GMLP + 3 * D_SB
N_EXPERTS = 32
TOP_K = 4
D_FF = D_MODEL
SWIGLU_LIMIT = 7.0
SWIGLU_ALPHA = 1.702
MOE_BLOCK = 512
EPS = 1e-5

kernel_name = "hymba_gmlp_stickbreaking_moe_block"


def rms_norm(x, w):
    xf = x.astype(jnp.float32)
    y = xf * lax.rsqrt(jnp.mean(xf * xf, axis=-1, keepdims=True) + EPS)
    return (y * w.astype(jnp.float32)).astype(x.dtype)


def group_rms_norm(x, w, groups):
    shp = x.shape
    xg = x.reshape(shp[:-1] + (groups, shp[-1] // groups))
    return rms_norm(xg, w.reshape(groups, -1)).reshape(shp)


def chunked_spatial_gating(uv, sgu_norm_w, sgu_w, sgu_b):
    B, S, _ = uv.shape
    z = jax.nn.gelu(uv)
    u, v = jnp.split(z, 2, axis=-1)
    v = group_rms_norm(v, sgu_norm_w, GMLP_GROUPS)
    v = v.reshape(B, S // CHUNK, CHUNK, GMLP_GROUPS, GMLP_GROUP_DIM)
    causal = jnp.tril(jnp.ones((CHUNK, CHUNK), sgu_w.dtype))
    w = sgu_w * causal[None]
    mixed = jnp.einsum("gts,bnsgc->bntgc", w, v) + sgu_b.T[:, :, None]
    return u * mixed.reshape(B, S, D_GMLP)


def stick_breaking_attention(q, k, v, q_norm_w, k_norm_w):
    B, S, _ = q.shape

    def heads(t):
        return t.reshape(B, S, SB_HEADS, SB_HEAD_DIM)

    qh = rms_norm(heads(q), q_norm_w).transpose(0, 2, 1, 3)
    kh = rms_norm(heads(k), k_norm_w).transpose(0, 2, 1, 3)
    vh = heads(v).transpose(0, 2, 1, 3)
    scale = SB_HEAD_DIM ** -0.5
    outs = []
    for i in range(S // Q_BLOCK):
        L = (i + 1) * Q_BLOCK
        qb = qh[:, :, i * Q_BLOCK:L]
        z = jnp.einsum("bhtd,bhsd->bhts", qb, kh[:, :, :L]).astype(jnp.float32) * scale
        t_pos = i * Q_BLOCK + jnp.arange(Q_BLOCK)
        s_pos = jnp.arange(L)
        causal = s_pos[None, :] < t_pos[:, None]
        log_beta = jax.nn.log_sigmoid(z)
        log_1m_beta = jnp.where(causal, jax.nn.log_sigmoid(-z), 0.0)
        survive = lax.cumsum(log_1m_beta, axis=log_1m_beta.ndim - 1, reverse=True) - log_1m_beta
        a = jnp.where(causal, jnp.exp(log_beta + survive), 0.0)
        outs.append(jnp.einsum("bhts,bhsd->bhtd", a.astype(vh.dtype), vh[:, :, :L]))
    o = jnp.concatenate(outs, axis=2)
    return o.transpose(0, 2, 1, 3).reshape(B, S, D_SB)


def moe(h, router_w, router_b, w_gate_up, b_gate_up, w_down, b_down):
    B, S, D = h.shape
    xt = h.reshape(-1, D)
    N = xt.shape[0]
    logits = xt.astype(jnp.float32) @ router_w.astype(jnp.float32) + router_b.astype(jnp.float32)
    top_logits, top_idx = lax.top_k(logits, TOP_K)
    gates = jax.nn.softmax(top_logits, axis=-1)
    n_assign = N * TOP_K
    flat_e = top_idx.reshape(-1)
    flat_tok = jnp.arange(n_assign, dtype=jnp.int32) // TOP_K
    order = jnp.argsort(flat_e)
    sorted_e = flat_e[order]
    counts = jnp.bincount(flat_e, length=N_EXPERTS)
    start = jnp.cumsum(counts) - counts
    padded = (counts + MOE_BLOCK - 1) // MOE_BLOCK * MOE_BLOCK
    pad_end = jnp.cumsum(padded)
    pad_start = pad_end - padded
    dest = pad_start[sorted_e] + (jnp.arange(n_assign) - start[sorted_e])
    n_blocks = -(-n_assign // MOE_BLOCK) + N_EXPERTS
    P = n_blocks * MOE_BLOCK
    slot_tok = jnp.zeros((P,), jnp.int32).at[dest].set(flat_tok[order])
    slot_w = jnp.zeros((P,), gates.dtype).at[dest].set(gates.reshape(-1)[order])
    block_start = jnp.arange(n_blocks) * MOE_BLOCK
    block_expert = jnp.minimum(jnp.searchsorted(pad_end, block_start, side="right"), N_EXPERTS - 1)

    def expert_block(args):
        tok, e = args
        xb = xt[tok]
        gu = xb @ w_gate_up[e] + b_gate_up[e]
        gate = jnp.minimum(gu[:, 0::2], SWIGLU_LIMIT)
        up = jnp.clip(gu[:, 1::2], -SWIGLU_LIMIT, SWIGLU_LIMIT)
        glu = gate * jax.nn.sigmoid(SWIGLU_ALPHA * gate)
        return ((up + 1.0) * glu) @ w_down[e] + b_down[e]

    y_slots = lax.map(expert_block, (slot_tok.reshape(n_blocks, MOE_BLOCK), block_expert))
    y_slots = y_slots.reshape(P, D).astype(jnp.float32) * slot_w[:, None]
    y = jax.ops.segment_sum(y_slots, slot_tok, num_segments=N)
    return y.reshape(B, S, D).astype(h.dtype)


def setup_inputs(seed: int = 0) -> dict:
    key = jax.random.key(seed)
    ks = jax.random.split(key, 18)
    f32 = jnp.float32

    def nrm(k, shape, scale):
        return jax.random.normal(k, shape, f32) * scale

    def gain(k, shape):
        return 1.0 + 0.05 * jax.random.normal(k, shape, f32)

    return {
        "x": jax.random.normal(ks[0], (BATCH, SEQ, D_MODEL), f32),
        "norm1_w": gain(ks[1], (DEPTH, D_MODEL)),
        "w_in": nrm(ks[2], (DEPTH, D_MODEL, D_IN), D_MODEL ** -0.5),
        "sgu_norm_w": gain(ks[3], (DEPTH, D_GMLP)),
        "sgu_w": nrm(ks[4], (DEPTH, GMLP_GROUPS, CHUNK, CHUNK), 0.5 * CHUNK ** -0.5),
        "sgu_b": gain(ks[5], (DEPTH, GMLP_GROUPS, CHUNK)),
        "q_norm_w": gain(ks[6], (DEPTH, SB_HEAD_DIM)),
        "k_norm_w": gain(ks[7], (DEPTH, SB_HEAD_DIM)),
        "out_norm_a_w": gain(ks[8], (DEPTH, D_GMLP)),
        "out_norm_b_w": gain(ks[9], (DEPTH, D_SB)),
        "w_out": nrm(ks[10], (DEPTH, D_MIX, D_MODEL), D_MIX ** -0.5),
        "norm2_w": gain(ks[11], (DEPTH, D_MODEL)),
        "router_w": nrm(ks[12], (DEPTH, D_MODEL, N_EXPERTS), D_MODEL ** -0.5),
        "router_b": nrm(ks[13], (DEPTH, N_EXPERTS), 0.01),
        "w_gate_up": nrm(ks[14], (DEPTH, N_EXPERTS, D_MODEL, 2 * D_FF), D_MODEL ** -0.5),
        "b_gate_up": nrm(ks[15], (DEPTH, N_EXPERTS, 2 * D_FF), 0.01),
        "w_down": nrm(ks[16], (DEPTH, N_EXPERTS, D_FF, D_MODEL), D_FF ** -0.5),
        "b_down": nrm(ks[17], (DEPTH, N_EXPERTS, D_MODEL), 0.01),
    }


def reference(x, norm1_w, w_in, sgu_norm_w, sgu_w, sgu_b, q_norm_w, k_norm_w,
              out_norm_a_w, out_norm_b_w, w_out, norm2_w, router_w, router_b,
              w_gate_up, b_gate_up, w_down, b_down):
    for l in range(DEPTH):
        h = rms_norm(x, norm1_w[l])
        proj = h @ w_in[l]
        uv, q, k, v = jnp.split(proj, [2 * D_GMLP, 2 * D_GMLP + D_SB, 2 * D_GMLP + 2 * D_SB], axis=-1)
        y_a = chunked_spatial_gating(uv, sgu_norm_w[l], sgu_w[l], sgu_b[l])
        y_b = stick_breaking_attention(q, k, v, q_norm_w[l], k_norm_w[l])
        y_a = group_rms_norm(y_a, out_norm_a_w[l], GMLP_GROUPS)
        y_b = group_rms_norm(y_b, out_norm_b_w[l], SB_HEADS)
        x = x + jnp.concatenate([y_a, y_b], axis=-1) @ w_out[l]
        x = x + moe(rms_norm(x, norm2_w[l]), router_w[l], router_b[l],
                    w_gate_up[l], b_gate_up[l], w_down[l], b_down[l])
    return x
```

```python
import functools

import jax
import jax.numpy as jnp
from jax import lax
from jax.experimental import pallas as pl
from jax.experimental.pallas import tpu as pltpu

F32 = jnp.float32
BF16 = jnp.bfloat16

D_MODEL = 1024
D_GMLP = 512
GMLP_GROUPS = 4
GROUP_DIM = D_GMLP // GMLP_GROUPS
CHUNK = 128
D_SB = 512
SB_HEADS = 8
HEAD_DIM = D_SB // SB_HEADS
N_EXPERTS = 32
TOP_K = 4
D_FF = 1024
SWIGLU_LIMIT = 7.0
SWIGLU_ALPHA = 1.702
EPS = 1e-5

LANES = 128
HEADS_PER_LANE_TILE = LANES // HEAD_DIM
N_HEAD_PAIRS = D_SB // LANES

ROW_TILE = 512
Q_TILE = 256
K_TILE = 256
MOE_BLOCK = 512
MOVE_TILE = 256
COMBINE_TILE = 128
NEG_BIG = -1e30
VMEM_LIMIT = 56 * 1024 * 1024


def _gelu_tanh(x):
    return 0.5 * x * (1.0 + jnp.tanh(0.7978845608028654 * (x + 0.044715 * (x * x * x))))


def _rms(x):
    return lax.rsqrt(jnp.mean(x * x, axis=-1, keepdims=True) + EPS)


def _inproj_kernel(x_ref, n1_ref, win_ref, sgn_ref, sgw_ref, sgbt_ref, ona_ref, qn_ref, kn_ref,
                   gm_ref, ya_ref, q_ref, k_ref, v_ref):
    x = x_ref[...]
    h = (x * _rms(x) * n1_ref[...]).astype(BF16)

    z = _gelu_tanh(jnp.dot(h, win_ref[:, :2 * D_GMLP], preferred_element_type=F32))
    row = lax.broadcasted_iota(jnp.int32, (CHUNK, CHUNK), 0)
    col = lax.broadcasted_iota(jnp.int32, (CHUNK, CHUNK), 1)
    tril = col <= row
    for g in range(GMLP_GROUPS):
        w = jnp.where(tril, sgw_ref[g], 0.0).astype(BF16)
        bias = sgbt_ref[:, g:g + 1]
        cu = slice(g * GROUP_DIM, (g + 1) * GROUP_DIM)
        cv = slice(D_GMLP + g * GROUP_DIM, D_GMLP + (g + 1) * GROUP_DIM)
        for c in range(ROW_TILE // CHUNK):
            rows = slice(c * CHUNK, (c + 1) * CHUNK)
            vg = z[rows, cv]
            vn = (vg * _rms(vg) * sgn_ref[:, cu]).astype(BF16)
            mixed = jnp.dot(w, vn, preferred_element_type=F32) + bias
            ya = z[rows, cu] * mixed
            ya_ref[rows, cu] = (ya * _rms(ya) * ona_ref[:, cu]).astype(BF16)

    qkv = jnp.dot(h, win_ref[:, 2 * D_GMLP:], preferred_element_type=F32)
    qp = qkv[:, :D_SB]
    kp = qkv[:, D_SB:2 * D_SB]
    gm = gm_ref[...]
    qms = jnp.dot((qp * qp).astype(BF16), gm, preferred_element_type=F32)
    kms = jnp.dot((kp * kp).astype(BF16), gm, preferred_element_type=F32)
    q_ref[...] = (qp * lax.rsqrt(qms + EPS) * qn_ref[...] * (HEAD_DIM ** -0.5)).astype(BF16)
    k_ref[...] = (kp * lax.rsqrt(kms + EPS) * kn_ref[...]).astype(BF16)
    v_ref[...] = qkv[:, 2 * D_SB:].astype(BF16)


def _inproj(x2, n1, win, sgn, sgw, sgbt, ona, qn, kn, gm):
    n = x2.shape[0]
    full = lambda shape: pl.BlockSpec(shape, lambda i: (0,) * len(shape))
    row_out = pl.BlockSpec((ROW_TILE, D_SB), lambda i: (i, 0))
    return pl.pallas_call(
        _inproj_kernel,
        grid=(n // ROW_TILE,),
        in_specs=[
            pl.BlockSpec((ROW_TILE, D_MODEL), lambda i: (i, 0)),
            full((1, D_MODEL)),
            full(win.shape),
            full((1, D_GMLP)),
            full(sgw.shape),
            full(sgbt.shape),
            full((1, D_GMLP)),
            full((1, D_SB)),
            full((1, D_SB)),
            full(gm.shape),
        ],
        out_specs=[row_out, row_out, row_out, row_out],
        out_shape=[jax.ShapeDtypeStruct((n, D_SB), BF16)] * 4,
        compiler_params=pltpu.CompilerParams(
            dimension_semantics=("arbitrary",), vmem_limit_bytes=VMEM_LIMIT),
        name="inproj_gmlp",
    )(x2, n1, win, sgn, sgw, sgbt, ona, qn, kn, gm)


def _attn_kernel(q_ref, k_ref, v_ref, u_ref, onb_ref, gp_ref, o_ref):
    qi = pl.program_id(2)
    q = q_ref[...]
    lane = lax.broadcasted_iota(jnp.int32, (Q_TILE, LANES), 1)
    first = lane < HEAD_DIM
    zero = jnp.zeros_like(q)
    q_heads = (jnp.where(first, q, zero), jnp.where(first, zero, q))
    u = u_ref[...]
    row = lax.broadcasted_iota(jnp.int32, (Q_TILE, K_TILE), 0)
    col = lax.broadcasted_iota(jnp.int32, (Q_TILE, K_TILE), 1)
    strict = col < row

    def key_block(j, carry, mask):
        start = pl.multiple_of(j * K_TILE, K_TILE)
        kj = k_ref[pl.ds(start, K_TILE), :]
        vj = v_ref[pl.ds(start, K_TILE), :]
        out = []
        for hd in range(HEADS_PER_LANE_TILE):
            acc, later = carry[hd]
            z = lax.dot_general(q_heads[hd], kj, (((1,), (1,)), ((), ())),
                                preferred_element_type=F32)
            log_beta = jnp.minimum(z, 0.0) - jnp.log(1.0 + jnp.exp(-jnp.abs(z)))
            log_rest = log_beta - z
            if mask is not None:
                log_rest = jnp.where(mask, log_rest, 0.0)
            within = jnp.dot(log_rest.astype(BF16), u, preferred_element_type=F32)
            a = jnp.exp(log_beta + within + later)
            if mask is not None:
                a = jnp.where(mask, a, 0.0)
            acc = acc + jnp.dot(a.astype(BF16), vj, preferred_element_type=F32)
            later = later + jnp.sum(log_rest, axis=-1, keepdims=True)
            out.append((acc, later))
        return tuple(out)

    init = tuple((jnp.zeros((Q_TILE, LANES), F32), jnp.zeros((Q_TILE, 1), F32))
                 for _ in range(HEADS_PER_LANE_TILE))
    carry = key_block(qi, init, strict)
    carry = lax.fori_loop(0, qi, lambda s, c: key_block(qi - 1 - s, c, None), carry)
    o = jnp.where(first, carry[0][0], carry[1][0])
    ms = jnp.dot((o * o).astype(BF16), gp_ref[...], preferred_element_type=F32)
    o_ref[...] = (o * lax.rsqrt(ms + EPS) * onb_ref[...]).astype(BF16)


def _attention(q, k, v, u, onb, gp, batch, seq):
    n = q.shape[0]
    nq = seq // Q_TILE
    q_spec = pl.BlockSpec((Q_TILE, LANES), lambda b, p, i: (b * nq + i, p))
    kv_spec = pl.BlockSpec((seq, LANES), lambda b, p, i: (b, p))
    return pl.pallas_call(
        _attn_kernel,
        grid=(batch, N_HEAD_PAIRS, nq),
        in_specs=[
            q_spec, kv_spec, kv_spec,
            pl.BlockSpec(u.shape, lambda b, p, i: (0, 0)),
            pl.BlockSpec((1, LANES), lambda b, p, i: (0, p)),
            pl.BlockSpec(gp.shape, lambda b, p, i: (0, 0)),
        ],
        out_specs=q_spec,
        out_shape=jax.ShapeDtypeStruct((n, D_SB), BF16),
        compiler_params=pltpu.CompilerParams(
            dimension_semantics=("arbitrary",) * 3, vmem_limit_bytes=VMEM_LIMIT),
        name="stickbreak_attn",
    )(q, k, v, u, onb, gp)


def _outproj_kernel(ya_ref, yb_ref, x_ref, wout_ref, n2_ref, rwh_ref, rwl_ref, rb_ref, ltri_ref,
                    x1_ref, h2_ref, route_ref, cnt_ref, seen_ref):
    @pl.when(pl.program_id(0) == 0)
    def _():
        seen_ref[...] = jnp.zeros_like(seen_ref)

    x1 = (x_ref[...]
          + jnp.dot(ya_ref[...], wout_ref[:D_GMLP, :], preferred_element_type=F32)
          + jnp.dot(yb_ref[...], wout_ref[D_GMLP:, :], preferred_element_type=F32))
    x1_ref[...] = x1
    h2 = x1 * _rms(x1) * n2_ref[...]
    h2_ref[...] = h2

    h_hi = h2.astype(BF16)
    h_lo = (h2 - h_hi.astype(F32)).astype(BF16)
    rwh = rwh_ref[...]
    logits = (jnp.dot(h_hi, rwh, preferred_element_type=F32)
              + jnp.dot(h_lo, rwh, preferred_element_type=F32)
              + jnp.dot(h_hi, rwl_ref[...], preferred_element_type=F32)
              + rb_ref[...])

    lane = lax.broadcasted_iota(jnp.int32, (ROW_TILE, LANES), 1)
    work = logits
    onehot = jnp.zeros((ROW_TILE, LANES), F32)
    top_logit, top_idx = [], []
    for _ in range(TOP_K):
        m = jnp.max(work, axis=-1, keepdims=True)
        idx = jnp.min(jnp.where(work == m, lane, LANES), axis=-1, keepdims=True)
        sel = lane == idx
        top_logit.append(m)
        top_idx.append(idx)
        onehot = jnp.where(sel, 1.0, onehot)
        work = jnp.where(sel, -jnp.inf, work)
    ex = [jnp.exp(m - top_logit[0]) for m in top_logit]
    denom = ex[0] + ex[1] + ex[2] + ex[3]

    seen = seen_ref[...]
    before = jnp.dot(ltri_ref[...], onehot.astype(BF16), preferred_element_type=F32) + seen
    seen_new = seen + jnp.sum(onehot, axis=0, keepdims=True)
    seen_ref[...] = seen_new
    cnt_ref[...] = jnp.broadcast_to(seen_new, cnt_ref.shape).astype(jnp.int32)

    route = jnp.zeros((ROW_TILE, LANES), jnp.int32)
    for k in range(TOP_K):
        rank = jnp.sum(jnp.where(lane == top_idx[k], before, 0.0), axis=-1, keepdims=True)
        gate = pltpu.bitcast(ex[k] / denom, jnp.int32)
        route = jnp.where(lane == k, top_idx[k], route)
        route = jnp.where(lane == TOP_K + k, rank.astype(jnp.int32), route)
        route = jnp.where(lane == 2 * TOP_K + k, gate, route)
    route_ref[...] = route


def _outproj(ya, yb, x2, wout, n2, rwh, rwl, rb, ltri):
    n = x2.shape[0]
    full = lambda shape: pl.BlockSpec(shape, lambda i: (0,) * len(shape))
    rows = lambda w: pl.BlockSpec((ROW_TILE, w), lambda i: (i, 0))
    return pl.pallas_call(
        _outproj_kernel,
        grid=(n // ROW_TILE,),
        in_specs=[rows(D_GMLP), rows(D_SB), rows(D_MODEL), full(wout.shape), full((1, D_MODEL)),
                  full(rwh.shape), full(rwl.shape), full((1, LANES)), full(ltri.shape)],
        out_specs=[rows(D_MODEL), rows(D_MODEL), rows(LANES), full((8, LANES))],
        out_shape=[jax.ShapeDtypeStruct((n, D_MODEL), F32),
                   jax.ShapeDtypeStruct((n, D_MODEL), F32),
                   jax.ShapeDtypeStruct((n, LANES), jnp.int32),
                   jax.ShapeDtypeStruct((8, LANES), jnp.int32)],
        scratch_shapes=[pltpu.VMEM((1, LANES), F32)],
        compiler_params=pltpu.CompilerParams(
            dimension_semantics=("arbitrary",), vmem_limit_bytes=VMEM_LIMIT),
        name="outproj_router",
    )(ya, yb, x2, wout, n2, rwh, rwl, rb, ltri)


def _dispatch_kernel(pend_ref, padded_ref, nused_ref, dest_hbm, h2_hbm, xs_hbm,
                     dest_smem, zero_buf, idx_sem, zero_sem, row_sem):
    i = pl.program_id(0)
    steps = pl.num_programs(0)
    n_blocks = xs_hbm.shape[0] // MOE_BLOCK
    rows_per_step = MOVE_TILE * TOP_K

    def idx_copy(step, slot):
        return pltpu.make_async_copy(dest_hbm.at[step], dest_smem.at[slot], idx_sem.at[slot])

    def zero_copy(start):
        return pltpu.make_async_copy(zero_buf, xs_hbm.at[pl.ds(start, MOE_BLOCK)], zero_sem)

    @pl.when(i == 0)
    def _():
        idx_copy(0, 0).start()
        zero_buf[...] = jnp.zeros_like(zero_buf)
        for e in range(N_EXPERTS):
            @pl.when(padded_ref[e] > 0)
            def _():
                zero_copy(pl.multiple_of(pend_ref[e] - MOE_BLOCK, MOE_BLOCK)).start()
        lax.fori_loop(nused_ref[0], n_blocks,
                      lambda b, c: (zero_copy(pl.multiple_of(b * MOE_BLOCK, MOE_BLOCK)).start(), c)[1], 0)
        for e in range(N_EXPERTS):
            @pl.when(padded_ref[e] > 0)
            def _():
                zero_copy(0).wait()
        lax.fori_loop(nused_ref[0], n_blocks, lambda b, c: (zero_copy(0).wait(), c)[1], 0)

    slot = i % 2
    idx_copy(i, slot).wait()

    @pl.when(i + 1 < steps)
    def _():
        idx_copy(i + 1, 1 - slot).start()

    def token(r, c):
        src = h2_hbm.at[pl.ds(i * MOVE_TILE + r, 1)]
        for k in range(TOP_K):
            d = dest_smem[slot, r * TOP_K + k]
            pltpu.make_async_copy(src, xs_hbm.at[pl.ds(d, 1)], row_sem).start()
        return c

    lax.fori_loop(0, MOVE_TILE, token, 0, unroll=8)
    pltpu.make_async_copy(h2_hbm.at[pl.ds(0, rows_per_step)],
                          xs_hbm.at[pl.ds(0, rows_per_step)], row_sem).wait()


def _dispatch(pad_end, padded, n_used, dest, h2, n_slots):
    n = h2.shape[0]
    steps = n // MOVE_TILE
    return pl.pallas_call(
        _dispatch_kernel,
        grid_spec=pltpu.PrefetchScalarGridSpec(
            num_scalar_prefetch=3,
            grid=(steps,),
            in_specs=[pl.BlockSpec(memory_space=pl.ANY), pl.BlockSpec(memory_space=pl.ANY)],
            out_specs=pl.BlockSpec(memory_space=pl.ANY),
            scratch_shapes=[
                pltpu.SMEM((2, MOVE_TILE * TOP_K), jnp.int32),
                pltpu.VMEM((MOE_BLOCK, D_MODEL), F32),
                pltpu.SemaphoreType.DMA((2,)),
                pltpu.SemaphoreType.DMA,
                pltpu.SemaphoreType.DMA,
            ]),
        out_shape=jax.ShapeDtypeStruct((n_slots, D_MODEL), F32),
        compiler_params=pltpu.CompilerParams(
            dimension_semantics=("arbitrary",), vmem_limit_bytes=VMEM_LIMIT),
        name="moe_dispatch",
    )(pad_end, padded, n_used, dest.reshape(steps, MOVE_TILE * TOP_K), h2)


def _expert_kernel(be_ref, nused_ref, xs_ref, wg_ref, wu_ref, wd_ref, bg_ref, bu_ref, bd_ref,
                   y_ref):
    b = pl.program_id(0)

    @pl.when(b < nused_ref[0])
    def _():
        x = xs_ref[...].astype(BF16)
        gate = jnp.dot(x, wg_ref[0], preferred_element_type=F32) + bg_ref[0]
        up = jnp.dot(x, wu_ref[0], preferred_element_type=F32) + bu_ref[0]
        gate = jnp.minimum(gate, SWIGLU_LIMIT)
        up = jnp.clip(up, -SWIGLU_LIMIT, SWIGLU_LIMIT)
        glu = gate / (1.0 + jnp.exp(-SWIGLU_ALPHA * gate))
        hidden = ((up + 1.0) * glu).astype(BF16)
        y_ref[...] = jnp.dot(hidden, wd_ref[0], preferred_element_type=F32) + bd_ref[0]

    @pl.when(b >= nused_ref[0])
    def _():
        y_ref[...] = jnp.zeros_like(y_ref)


def _experts(block_expert, n_used, xs, wg, wu, wd, bg, bu, bd):
    n_blocks = xs.shape[0] // MOE_BLOCK
    w_spec = lambda: pl.BlockSpec((1, D_MODEL, D_FF), lambda b, be, nu: (be[b], 0, 0))
    b_spec = lambda: pl.BlockSpec((1, 1, D_FF), lambda b, be, nu: (be[b], 0, 0))
    return pl.pallas_call(
        _expert_kernel,
        grid_spec=pltpu.PrefetchScalarGridSpec(
            num_scalar_prefetch=2,
            grid=(n_blocks,),
            in_specs=[
                pl.BlockSpec((MOE_BLOCK, D_MODEL),
                             lambda b, be, nu: (jnp.minimum(b, jnp.maximum(nu[0] - 1, 0)), 0)),
                w_spec(), w_spec(), w_spec(), b_spec(), b_spec(), b_spec(),
            ],
            out_specs=pl.BlockSpec((MOE_BLOCK, D_MODEL), lambda b, be, nu: (b, 0))),
        out_shape=jax.ShapeDtypeStruct(xs.shape, F32),
        compiler_params=pltpu.CompilerParams(
            dimension_semantics=("arbitrary",), vmem_limit_bytes=VMEM_LIMIT),
        name="moe_experts",
    )(block_expert, n_used, xs, wg, wu, wd, bg, bu, bd)


def _combine_kernel(dest_hbm, y_hbm, x1_ref, route_ref, out_ref, dest_smem, ybuf, idx_sem, row_sem):
    i = pl.program_id(0)
    steps = pl.num_programs(0)

    def idx_copy(step, slot):
        return pltpu.make_async_copy(dest_hbm.at[step], dest_smem.at[slot], idx_sem.at[slot])

    def start_rows(slot):
        def token(r, c):
            for k in range(TOP_K):
                d = dest_smem[slot, r * TOP_K + k]
                pltpu.make_async_copy(y_hbm.at[pl.ds(d, 1)], ybuf.at[slot, k, pl.ds(r, 1)],
                                      row_sem.at[slot]).start()
            return c
        lax.fori_loop(0, COMBINE_TILE, token, 0, unroll=8)

    slot = i % 2

    @pl.when(i == 0)
    def _():
        idx_copy(0, 0).start()
        idx_copy(0, 0).wait()
        start_rows(0)

        @pl.when(steps > 1)
        def _():
            idx_copy(1, 1).start()

    @pl.when(i + 1 < steps)
    def _():
        idx_copy(i + 1, 1 - slot).wait()
        start_rows(1 - slot)

        @pl.when(i + 2 < steps)
        def _():
            idx_copy(i + 2, slot).start()

    for k in range(TOP_K):
        pltpu.make_async_copy(y_hbm.at[pl.ds(0, COMBINE_TILE)], ybuf.at[slot, k],
                              row_sem.at[slot]).wait()
    acc = x1_ref[...]
    route = route_ref[...]
    for k in range(TOP_K):
        gate = pltpu.bitcast(route[:, 2 * TOP_K + k:2 * TOP_K + k + 1], F32)
        acc = acc + gate * ybuf[slot, k]
    out_ref[...] = acc


def _combine(dest, y_slots, x1, route):
    n = x1.shape[0]
    steps = n // COMBINE_TILE
    rows = lambda w: pl.BlockSpec((COMBINE_TILE, w), lambda i: (i, 0))
    return pl.pallas_call(
        _combine_kernel,
        grid=(steps,),
        in_specs=[pl.BlockSpec(memory_space=pl.ANY), pl.BlockSpec(memory_space=pl.ANY),
                  rows(D_MODEL), rows(LANES)],
        out_specs=rows(D_MODEL),
        out_shape=jax.ShapeDtypeStruct((n, D_MODEL), F32),
        scratch_shapes=[
            pltpu.SMEM((2, COMBINE_TILE * TOP_K), jnp.int32),
            pltpu.VMEM((2, TOP_K, COMBINE_TILE, D_MODEL), F32),
            pltpu.SemaphoreType.DMA((2,)),
            pltpu.SemaphoreType.DMA((2,)),
        ],
        compiler_params=pltpu.CompilerParams(
            dimension_semantics=("arbitrary",), vmem_limit_bytes=VMEM_LIMIT),
        name="moe_combine",
    )(dest.reshape(steps, COMBINE_TILE * TOP_K), y_slots, x1, route)


def _block_diag_mean(width, group):
    r = jnp.arange(width)[:, None] // group
    c = jnp.arange(width)[None, :] // group
    return jnp.where(r == c, 1.0 / group, 0.0).astype(BF16)


def _layer(x2, batch, seq, norm1_w, w_in, sgu_norm_w, sgu_w, sgu_b, q_norm_w, k_norm_w,
           out_norm_a_w, out_norm_b_w, w_out, norm2_w, router_w, router_b,
           w_gate_up, b_gate_up, w_down, b_down):
    n = x2.shape[0]
    row = lambda a: a.reshape(1, -1).astype(F32)

    ya, q, k, v = _inproj(
        x2, row(norm1_w), w_in.astype(BF16), row(sgu_norm_w), sgu_w.astype(F32),
        sgu_b.T.astype(F32), row(out_norm_a_w), row(jnp.tile(q_norm_w, SB_HEADS)),
        row(jnp.tile(k_norm_w, SB_HEADS)), _block_diag_mean(D_SB, HEAD_DIM))

    j = jnp.arange(K_TILE)
    later_keys = (j[:, None] > j[None, :]).astype(BF16)
    yb = _attention(q, k, v, later_keys, row(out_norm_b_w), _block_diag_mean(LANES, HEAD_DIM),
                    batch, seq)

    rw = jnp.zeros((D_MODEL, LANES), F32).at[:, :N_EXPERTS].set(router_w.astype(F32))
    rw_hi = rw.astype(BF16)
    rw_lo = (rw - rw_hi.astype(F32)).astype(BF16)
    rb = jnp.full((1, LANES), NEG_BIG, F32).at[0, :N_EXPERTS].set(router_b.astype(F32))
    t = jnp.arange(ROW_TILE)
    earlier_rows = (t[None, :] < t[:, None]).astype(BF16)
    x1, h2, route, counts = _outproj(ya, yb, x2, w_out.astype(BF16), row(norm2_w),
                                     rw_hi, rw_lo, rb, earlier_rows)

    counts = counts[0, :N_EXPERTS]
    padded = (counts + MOE_BLOCK - 1) // MOE_BLOCK * MOE_BLOCK
    pad_end = jnp.cumsum(padded)
    pad_start = pad_end - padded
    n_blocks = -(-(n * TOP_K) // MOE_BLOCK) + N_EXPERTS
    n_used = (pad_end[-1] // MOE_BLOCK).astype(jnp.int32)
    block_start = jnp.arange(n_blocks, dtype=jnp.int32) * MOE_BLOCK
    block_expert = jnp.minimum(jnp.searchsorted(pad_end, block_start, side="right"),
                               N_EXPERTS - 1).astype(jnp.int32)
    last_expert = block_expert[jnp.maximum(n_used - 1, 0)]
    block_expert = jnp.where(jnp.arange(n_blocks) < n_used, block_expert, last_expert)
    dest = (pad_start[route[:, :TOP_K]] + route[:, TOP_K:2 * TOP_K]).astype(jnp.int32)
    n_used = n_used.reshape(1)

    xs = _dispatch(pad_end.astype(jnp.int32), padded.astype(jnp.int32), n_used, dest, h2,
                   n_blocks * MOE_BLOCK)
    y_slots = _experts(
        block_expert, n_used, xs,
        w_gate_up[:, :, 0::2].astype(BF16), w_gate_up[:, :, 1::2].astype(BF16), w_down.astype(BF16),
        b_gate_up[:, None, 0::2].astype(F32), b_gate_up[:, None, 1::2].astype(F32),
        b_down[:, None, :].astype(F32))
    return _combine(dest, y_slots, x1, route)


def kernel(x, norm1_w, w_in, sgu_norm_w, sgu_w, sgu_b, q_norm_w, k_norm_w, out_norm_a_w,
           out_norm_b_w, w_out, norm2_w, router_w, router_b, w_gate_up, b_gate_up, w_down, b_down):
    batch, seq, d = x.shape
    assert d == D_MODEL and seq % max(ROW_TILE, Q_TILE, K_TILE) == 0 and Q_TILE == K_TILE
    x2 = x.reshape(batch * seq, d)
    for l in range(norm1_w.shape[0]):
        x2 = _layer(x2, batch, seq, norm1_w[l], w_in[l], sgu_norm_w[l], sgu_w[l], sgu_b[l],
                    q_norm_w[l], k_norm_w[l], out_norm_a_w[l], out_norm_b_w[l], w_out[l],
                    norm2_w[l], router_w[l], router_b[l], w_gate_up[l], b_gate_up[l], w_down[l],
                    b_down[l])
    return x2.reshape(batch, seq, d)
```

```python
import jax
import jax.numpy as jnp
from jax import lax
from jax.experimental import pallas as pl
from jax.experimental.pallas import tpu as pltpu

F32 = jnp.float32
BF16 = jnp.bfloat16

D_MODEL = 1024
D_GMLP = 512
GMLP_GROUPS = 4
GROUP_DIM = D_GMLP // GMLP_GROUPS
CHUNK = 128
D_SB = 512
SB_HEADS = 8
HEAD_DIM = D_SB // SB_HEADS
N_EXPERTS = 32
TOP_K = 4
D_FF = 1024
SWIGLU_LIMIT = 7.0
SWIGLU_ALPHA = 1.702
EPS = 1e-5

LANES = 128
LANE_TILES = D_MODEL // LANES
Y_ROWS = LANE_TILES
X_ROWS = LANE_TILES // 2
HEADS_PER_LANE_TILE = LANES // HEAD_DIM
N_HEAD_PAIRS = D_SB // LANES

ROW_TILE = 512
Q_TILE = 256
K_TILE = 256
MOE_BLOCK = 512
MOVE_TILE = 256
COMBINE_TILE = 256
IDX_WINDOW = MOVE_TILE * TOP_K
assert COMBINE_TILE * TOP_K == IDX_WINDOW
SPLIT_ROWS = 256
SPLIT_COLS = 256
NEG_BIG = -1e30
LOG2E = 1.4426950408889634
VMEM_LIMIT = 56 * 1024 * 1024


def _gelu_tanh(x):
    return 0.5 * x * (1.0 + jnp.tanh(0.7978845608028654 * (x + 0.044715 * (x * x * x))))


def _rms(x):
    return lax.rsqrt(jnp.mean(x * x, axis=-1, keepdims=True) + EPS)


def _slot_window(slot):
    return pl.ds(pl.multiple_of(slot * IDX_WINDOW, IDX_WINDOW), IDX_WINDOW)


def _inproj_kernel(x_ref, n1_ref, win_ref, sgn_ref, sgw_ref, sgbt_ref, ona_ref, qn_ref, kn_ref,
                   gm_ref, ya_ref, q_ref, k_ref, v_ref):
    x = x_ref[...]
    h = (x * _rms(x) * n1_ref[...]).astype(BF16)

    z = _gelu_tanh(jnp.dot(h, win_ref[:, :2 * D_GMLP], preferred_element_type=F32))
    row = lax.broadcasted_iota(jnp.int32, (CHUNK, CHUNK), 0)
    col = lax.broadcasted_iota(jnp.int32, (CHUNK, CHUNK), 1)
    tril = col <= row
    for g in range(GMLP_GROUPS):
        w = jnp.where(tril, sgw_ref[g], 0.0).astype(BF16)
        bias = sgbt_ref[:, g:g + 1]
        cu = slice(g * GROUP_DIM, (g + 1) * GROUP_DIM)
        cv = slice(D_GMLP + g * GROUP_DIM, D_GMLP + (g + 1) * GROUP_DIM)
        for c in range(ROW_TILE // CHUNK):
            rows = slice(c * CHUNK, (c + 1) * CHUNK)
            vg = z[rows, cv]
            vn = (vg * _rms(vg) * sgn_ref[:, cu]).astype(BF16)
            mixed = jnp.dot(w, vn, preferred_element_type=F32) + bias
            ya = z[rows, cu] * mixed
            ya_ref[rows, cu] = (ya * _rms(ya) * ona_ref[:, cu]).astype(BF16)

    qkv = jnp.dot(h, win_ref[:, 2 * D_GMLP:], preferred_element_type=F32)
    qp = qkv[:, :D_SB]
    kp = qkv[:, D_SB:2 * D_SB]
    gm = gm_ref[...]
    qms = jnp.dot((qp * qp).astype(BF16), gm, preferred_element_type=F32)
    kms = jnp.dot((kp * kp).astype(BF16), gm, preferred_element_type=F32)
    q_ref[...] = (qp * lax.rsqrt(qms + EPS) * qn_ref[...] * (HEAD_DIM ** -0.5 * LOG2E)).astype(BF16)
    k_ref[...] = (kp * lax.rsqrt(kms + EPS) * kn_ref[...]).astype(BF16)
    v_ref[...] = qkv[:, 2 * D_SB:].astype(BF16)


def _inproj(x2, n1, win, sgn, sgw, sgbt, ona, qn, kn, gm):
    n = x2.shape[0]
    full = lambda shape: pl.BlockSpec(shape, lambda i: (0,) * len(shape))
    row_out = pl.BlockSpec((ROW_TILE, D_SB), lambda i: (i, 0))
    return pl.pallas_call(
        _inproj_kernel,
        grid=(n // ROW_TILE,),
        in_specs=[
            pl.BlockSpec((ROW_TILE, D_MODEL), lambda i: (i, 0)),
            full((1, D_MODEL)),
            full(win.shape),
            full((1, D_GMLP)),
            full(sgw.shape),
            full(sgbt.shape),
            full((1, D_GMLP)),
            full((1, D_SB)),
            full((1, D_SB)),
            full(gm.shape),
        ],
        out_specs=[row_out, row_out, row_out, row_out],
        out_shape=[jax.ShapeDtypeStruct((n, D_SB), BF16)] * 4,
        compiler_params=pltpu.CompilerParams(
            dimension_semantics=("arbitrary",), vmem_limit_bytes=VMEM_LIMIT),
        name="inproj_gmlp",
    )(x2, n1, win, sgn, sgw, sgbt, ona, qn, kn, gm)


def _attn_kernel(q_ref, k_ref, v_ref, u_ref, onb_ref, gp_ref, o_ref):
    qi = pl.program_id(2)
    q = q_ref[...]
    lane = lax.broadcasted_iota(jnp.int32, (Q_TILE, LANES), 1)
    first = lane < HEAD_DIM
    zero = jnp.zeros_like(q)
    q_heads = (jnp.where(first, q, zero), jnp.where(first, zero, q))
    u = u_ref[...]
    row = lax.broadcasted_iota(jnp.int32, (Q_TILE, K_TILE), 0)
    col = lax.broadcasted_iota(jnp.int32, (Q_TILE, K_TILE), 1)
    strict = col < row

    def key_block(j, carry, mask):
        start = pl.multiple_of(j * K_TILE, K_TILE)
        kj = k_ref[pl.ds(start, K_TILE), :]
        vj = v_ref[pl.ds(start, K_TILE), :]
        out = []
        for hd in range(HEADS_PER_LANE_TILE):
            acc, later = carry[hd]
            z = lax.dot_general(q_heads[hd], kj, (((1,), (1,)), ((), ())),
                                preferred_element_type=F32)
            log_beta = jnp.minimum(z, 0.0) - jnp.log(1.0 + jnp.exp2(-jnp.abs(z))) * LOG2E
            log_rest = log_beta - z
            if mask is not None:
                log_rest = jnp.where(mask, log_rest, 0.0)
            within = jnp.dot(log_rest.astype(BF16), u, preferred_element_type=F32)
            a = jnp.exp2(log_beta + within + later)
            if mask is not None:
                a = jnp.where(mask, a, 0.0)
            acc = acc + jnp.dot(a.astype(BF16), vj, preferred_element_type=F32)
            later = later + jnp.sum(log_rest, axis=-1, keepdims=True)
            out.append((acc, later))
        return tuple(out)

    init = tuple((jnp.zeros((Q_TILE, LANES), F32), jnp.zeros((Q_TILE, 1), F32))
                 for _ in range(HEADS_PER_LANE_TILE))
    carry = key_block(qi, init, strict)
    odd = qi % 2
    carry = lax.fori_loop(0, odd, lambda s, c: key_block(qi - 1, c, None), carry)
    top = qi - 1 - odd
    carry = lax.fori_loop(
        0, (qi - odd) // 2,
        lambda s, c: key_block(top - 2 * s - 1, key_block(top - 2 * s, c, None), None), carry)
    o = jnp.where(first, carry[0][0], carry[1][0])
    ms = jnp.dot((o * o).astype(BF16), gp_ref[...], preferred_element_type=F32)
    o_ref[...] = (o * lax.rsqrt(ms + EPS) * onb_ref[...]).astype(BF16)


def _attention(q, k, v, u, onb, gp, batch, seq):
    n = q.shape[0]
    nq = seq // Q_TILE
    q_spec = pl.BlockSpec((Q_TILE, LANES), lambda b, p, i: (b * nq + i, p))
    kv_spec = pl.BlockSpec((seq, LANES), lambda b, p, i: (b, p))
    return pl.pallas_call(
        _attn_kernel,
        grid=(batch, N_HEAD_PAIRS, nq),
        in_specs=[
            q_spec, kv_spec, kv_spec,
            pl.BlockSpec(u.shape, lambda b, p, i: (0, 0)),
            pl.BlockSpec((1, LANES), lambda b, p, i: (0, p)),
            pl.BlockSpec(gp.shape, lambda b, p, i: (0, 0)),
        ],
        out_specs=q_spec,
        out_shape=jax.ShapeDtypeStruct((n, D_SB), BF16),
        compiler_params=pltpu.CompilerParams(
            dimension_semantics=("arbitrary",) * 3, vmem_limit_bytes=VMEM_LIMIT),
        name="stickbreak_attn",
    )(q, k, v, u, onb, gp)


def _outproj_kernel(ya_ref, yb_ref, x_ref, wout_ref, n2_ref, rwh_ref, rwl_ref, rb_ref, ltri_ref,
                    x1_ref, h2p_ref, gate_ref, route_t_ref, cnt_ref, seen_ref, tok_ref):
    @pl.when(pl.program_id(0) == 0)
    def _():
        seen_ref[...] = jnp.zeros_like(seen_ref)

    x1 = (x_ref[...]
          + jnp.dot(ya_ref[...], wout_ref[:D_GMLP, :], preferred_element_type=F32)
          + jnp.dot(yb_ref[...], wout_ref[D_GMLP:, :], preferred_element_type=F32))
    x1_ref[...] = x1
    h2 = x1 * _rms(x1) * n2_ref[...]
    h_hi = h2.astype(BF16)

    for c in range(Y_ROWS):
        tok_ref[pl.ds(c, ROW_TILE, stride=Y_ROWS), :] = h2[:, c * LANES:(c + 1) * LANES]
    h2p_ref[...] = pltpu.bitcast(tok_ref[...].astype(BF16), jnp.uint32)

    h_lo = (h2 - h_hi.astype(F32)).astype(BF16)
    rwh = rwh_ref[...]
    logits = (jnp.dot(h_hi, rwh, preferred_element_type=F32)
              + jnp.dot(h_lo, rwh, preferred_element_type=F32)
              + jnp.dot(h_hi, rwl_ref[...], preferred_element_type=F32)
              + rb_ref[...])

    lane = lax.broadcasted_iota(jnp.int32, (ROW_TILE, LANES), 1)
    work = logits
    onehot = jnp.zeros((ROW_TILE, LANES), F32)
    top_logit, top_idx = [], []
    for _ in range(TOP_K):
        m = jnp.max(work, axis=-1, keepdims=True)
        idx = jnp.min(jnp.where(work == m, lane, LANES), axis=-1, keepdims=True)
        sel = lane == idx
        top_logit.append(m)
        top_idx.append(idx)
        onehot = jnp.where(sel, 1.0, onehot)
        work = jnp.where(sel, -jnp.inf, work)
    ex = [jnp.exp(m - top_logit[0]) for m in top_logit]
    denom = ex[0] + ex[1] + ex[2] + ex[3]

    seen = seen_ref[...]
    before = jnp.dot(ltri_ref[...], onehot.astype(BF16), preferred_element_type=F32) + seen
    seen_new = seen + jnp.sum(onehot, axis=0, keepdims=True)
    seen_ref[...] = seen_new
    cnt_ref[...] = jnp.broadcast_to(seen_new, cnt_ref.shape).astype(jnp.int32)

    route = jnp.zeros((ROW_TILE, LANES), jnp.int32)
    gates = jnp.zeros((ROW_TILE, LANES), F32)
    for k in range(TOP_K):
        rank = jnp.sum(jnp.where(lane == top_idx[k], before, 0.0), axis=-1, keepdims=True)
        route = jnp.where(lane == k, top_idx[k], route)
        route = jnp.where(lane == TOP_K + k, rank.astype(jnp.int32), route)
        gates = jnp.where(lane == k, ex[k] / denom, gates)
    gate_ref[...] = gates
    route_t_ref[0] = route.T[:2 * TOP_K, :]


def _outproj(ya, yb, x2, wout, n2, rwh, rwl, rb, ltri):
    n = x2.shape[0]
    steps = n // ROW_TILE
    full = lambda shape: pl.BlockSpec(shape, lambda i: (0,) * len(shape))
    rows = lambda w: pl.BlockSpec((ROW_TILE, w), lambda i: (i, 0))
    return pl.pallas_call(
        _outproj_kernel,
        grid=(steps,),
        in_specs=[rows(D_GMLP), rows(D_SB), rows(D_MODEL), full(wout.shape), full((1, D_MODEL)),
                  full(rwh.shape), full(rwl.shape), full((1, LANES)), full(ltri.shape)],
        out_specs=[rows(D_MODEL),
                   pl.BlockSpec((ROW_TILE * X_ROWS, LANES), lambda i: (i, 0)),
                   rows(LANES),
                   pl.BlockSpec((1, 2 * TOP_K, ROW_TILE), lambda i: (i, 0, 0)),
                   full((8, LANES))],
        out_shape=[jax.ShapeDtypeStruct((n, D_MODEL), F32),
                   jax.ShapeDtypeStruct((n * X_ROWS, LANES), jnp.uint32),
                   jax.ShapeDtypeStruct((n, LANES), F32),
                   jax.ShapeDtypeStruct((steps, 2 * TOP_K, ROW_TILE), jnp.int32),
                   jax.ShapeDtypeStruct((8, LANES), jnp.int32)],
        scratch_shapes=[pltpu.VMEM((1, LANES), F32), pltpu.VMEM((ROW_TILE * Y_ROWS, LANES), F32)],
        compiler_params=pltpu.CompilerParams(
            dimension_semantics=("arbitrary",), vmem_limit_bytes=VMEM_LIMIT),
        name="outproj_router",
    )(ya, yb, x2, wout, n2, rwh, rwl, rb, ltri)


def _split_kernel(w_ref, perm_ref, wg_ref, wu_ref):
    perm = perm_ref[...]
    half = SPLIT_COLS // 2
    for c in range(2 * D_FF // SPLIT_COLS):
        chunk = w_ref[0, :, c * SPLIT_COLS:(c + 1) * SPLIT_COLS].astype(BF16)
        r = jnp.dot(chunk, perm, preferred_element_type=F32)
        wg_ref[0, :, c * half:(c + 1) * half] = r[:, :half].astype(BF16)
        wu_ref[0, :, c * half:(c + 1) * half] = r[:, half:].astype(BF16)


def _split_gate_up(w_gate_up):
    e, d, f2 = w_gate_up.shape
    j = jnp.arange(SPLIT_COLS)
    target = jnp.where(j % 2 == 0, j // 2, SPLIT_COLS // 2 + j // 2)
    perm = (target[:, None] == j[None, :]).astype(BF16)
    out_spec = pl.BlockSpec((1, SPLIT_ROWS, f2 // 2), lambda i, r: (i, r, 0))
    return pl.pallas_call(
        _split_kernel,
        grid=(e, d // SPLIT_ROWS),
        in_specs=[pl.BlockSpec((1, SPLIT_ROWS, f2), lambda i, r: (i, r, 0)),
                  pl.BlockSpec(perm.shape, lambda i, r: (0, 0))],
        out_specs=[out_spec, out_spec],
        out_shape=[jax.ShapeDtypeStruct((e, d, f2 // 2), BF16)] * 2,
        compiler_params=pltpu.CompilerParams(
            dimension_semantics=("arbitrary",) * 2, vmem_limit_bytes=VMEM_LIMIT),
        name="split_gate_up",
    )(w_gate_up, perm)


def _dispatch_kernel(pend_ref, padded_ref, nused_ref, dest_hbm, h2p_ref, xs_hbm,
                     dest_smem, zero_buf, idx_sem, zero_sem, row_sem):
    i = pl.program_id(0)
    steps = pl.num_programs(0)
    block_rows = MOE_BLOCK * X_ROWS
    n_blocks = xs_hbm.shape[0] // block_rows

    def idx_copy(step, slot):
        return pltpu.make_async_copy(dest_hbm.at[step], dest_smem.at[_slot_window(slot)],
                                     idx_sem.at[slot])

    def zero_copy(start):
        return pltpu.make_async_copy(zero_buf, xs_hbm.at[pl.ds(start, block_rows)], zero_sem)

    @pl.when(i == 0)
    def _():
        idx_copy(0, 0).start()
        zero_buf[...] = jnp.zeros_like(zero_buf)
        for e in range(N_EXPERTS):
            @pl.when(padded_ref[e] > 0)
            def _():
                last_block = pend_ref[e] // MOE_BLOCK - 1
                zero_copy(pl.multiple_of(last_block * block_rows, block_rows)).start()
        lax.fori_loop(
            nused_ref[0], n_blocks,
            lambda b, c: (zero_copy(pl.multiple_of(b * block_rows, block_rows)).start(), c)[1], 0)
        for e in range(N_EXPERTS):
            @pl.when(padded_ref[e] > 0)
            def _():
                zero_copy(0).wait()
        lax.fori_loop(nused_ref[0], n_blocks, lambda b, c: (zero_copy(0).wait(), c)[1], 0)

    slot = i % 2
    idx_copy(i, slot).wait()

    @pl.when(i + 1 < steps)
    def _():
        idx_copy(i + 1, 1 - slot).start()

    def token(r, c):
        src = h2p_ref.at[pl.ds(pl.multiple_of(r * X_ROWS, X_ROWS), X_ROWS)]
        for k in range(TOP_K):
            d = pl.multiple_of(dest_smem[slot * IDX_WINDOW + k * MOVE_TILE + r], X_ROWS)
            pltpu.make_async_copy(src, xs_hbm.at[pl.ds(d, X_ROWS)], row_sem).start()
        return c

    lax.fori_loop(0, MOVE_TILE, token, 0, unroll=8)
    for k in range(TOP_K):
        pltpu.make_async_copy(h2p_ref, xs_hbm.at[pl.ds(0, MOVE_TILE * X_ROWS)], row_sem).wait()


def _dispatch(pad_end, padded, n_used, dest, h2p, n_slots):
    steps = h2p.shape[0] // (MOVE_TILE * X_ROWS)
    return pl.pallas_call(
        _dispatch_kernel,
        grid_spec=pltpu.PrefetchScalarGridSpec(
            num_scalar_prefetch=3,
            grid=(steps,),
            in_specs=[pl.BlockSpec(memory_space=pl.ANY),
                      pl.BlockSpec((MOVE_TILE * X_ROWS, LANES), lambda i, pe, pd, nu: (i, 0))],
            out_specs=pl.BlockSpec(memory_space=pl.ANY),
            scratch_shapes=[
                pltpu.SMEM((2 * IDX_WINDOW,), jnp.int32),
                pltpu.VMEM((MOE_BLOCK * X_ROWS, LANES), jnp.uint32),
                pltpu.SemaphoreType.DMA((2,)),
                pltpu.SemaphoreType.DMA,
                pltpu.SemaphoreType.DMA,
            ]),
        out_shape=jax.ShapeDtypeStruct((n_slots * X_ROWS, LANES), jnp.uint32),
        compiler_params=pltpu.CompilerParams(
            dimension_semantics=("arbitrary",), vmem_limit_bytes=VMEM_LIMIT),
        name="moe_dispatch",
    )(pad_end, padded, n_used, dest, h2p)


def _expert_kernel(be_ref, nused_ref, xs_ref, wg_ref, wu_ref, wd_ref, bg_ref, bu_ref, bd_ref,
                   y_ref, tok_ref):
    b = pl.program_id(0)

    @pl.when(b < nused_ref[0])
    def _():
        tok_ref[...] = pltpu.bitcast(xs_ref[...], BF16).astype(F32)
        x = jnp.concatenate(
            [tok_ref[pl.ds(c, MOE_BLOCK, stride=Y_ROWS), :].astype(BF16) for c in range(Y_ROWS)],
            axis=-1)
        gate = jnp.dot(x, wg_ref[0], preferred_element_type=F32) + bg_ref[0]
        up = jnp.dot(x, wu_ref[0], preferred_element_type=F32) + bu_ref[0]
        gate = jnp.minimum(gate, SWIGLU_LIMIT)
        up = jnp.clip(up, -SWIGLU_LIMIT, SWIGLU_LIMIT)
        glu = gate / (1.0 + jnp.exp(-SWIGLU_ALPHA * gate))
        hidden = ((up + 1.0) * glu).astype(BF16)
        y = jnp.dot(hidden, wd_ref[0], preferred_element_type=F32) + bd_ref[0]
        for c in range(Y_ROWS):
            y_ref[pl.ds(c, MOE_BLOCK, stride=Y_ROWS), :] = y[:, c * LANES:(c + 1) * LANES]

    @pl.when(b >= nused_ref[0])
    def _():
        y_ref[...] = jnp.zeros_like(y_ref)


def _experts(block_expert, n_used, xs, wg, wu, wd, bg, bu, bd):
    n_blocks = xs.shape[0] // (MOE_BLOCK * X_ROWS)
    w_spec = lambda: pl.BlockSpec((1, D_MODEL, D_FF), lambda b, be, nu: (be[b], 0, 0))
    b_spec = lambda: pl.BlockSpec((1, 1, D_FF), lambda b, be, nu: (be[b], 0, 0))
    return pl.pallas_call(
        _expert_kernel,
        grid_spec=pltpu.PrefetchScalarGridSpec(
            num_scalar_prefetch=2,
            grid=(n_blocks,),
            in_specs=[
                pl.BlockSpec((MOE_BLOCK * X_ROWS, LANES),
                             lambda b, be, nu: (jnp.minimum(b, jnp.maximum(nu[0] - 1, 0)), 0)),
                w_spec(), w_spec(), w_spec(), b_spec(), b_spec(), b_spec(),
            ],
            out_specs=pl.BlockSpec((MOE_BLOCK * Y_ROWS, LANES), lambda b, be, nu: (b, 0)),
            scratch_shapes=[pltpu.VMEM((MOE_BLOCK * Y_ROWS, LANES), F32)]),
        out_shape=jax.ShapeDtypeStruct((n_blocks * MOE_BLOCK * Y_ROWS, LANES), F32),
        compiler_params=pltpu.CompilerParams(
            dimension_semantics=("arbitrary",), vmem_limit_bytes=VMEM_LIMIT),
        name="moe_experts",
    )(block_expert, n_used, xs, wg, wu, wd, bg, bu, bd)


def _combine_kernel(dest_hbm, y_hbm, x1_ref, gate_ref, out_ref, dest_smem, ybuf, idx_sem, row_sem):
    i = pl.program_id(0)
    steps = pl.num_programs(0)

    def idx_copy(step, slot):
        return pltpu.make_async_copy(dest_hbm.at[step], dest_smem.at[_slot_window(slot)],
                                     idx_sem.at[slot])

    def start_rows(slot):
        def token(r, c):
            for k in range(TOP_K):
                d = pl.multiple_of(dest_smem[slot * IDX_WINDOW + k * COMBINE_TILE + r], Y_ROWS)
                dst = ybuf.at[slot, k, pl.ds(pl.multiple_of(r * Y_ROWS, Y_ROWS), Y_ROWS)]
                pltpu.make_async_copy(y_hbm.at[pl.ds(d, Y_ROWS)], dst, row_sem.at[slot]).start()
            return c
        lax.fori_loop(0, COMBINE_TILE, token, 0, unroll=8)

    slot = i % 2

    @pl.when(i == 0)
    def _():
        idx_copy(0, 0).start()
        idx_copy(0, 0).wait()
        start_rows(0)

        @pl.when(steps > 1)
        def _():
            idx_copy(1, 1).start()

    @pl.when(i + 1 < steps)
    def _():
        idx_copy(i + 1, 1 - slot).wait()
        start_rows(1 - slot)

        @pl.when(i + 2 < steps)
        def _():
            idx_copy(i + 2, slot).start()

    for k in range(TOP_K):
        pltpu.make_async_copy(y_hbm.at[pl.ds(0, COMBINE_TILE * Y_ROWS)], ybuf.at[slot, k],
                              row_sem.at[slot]).wait()
    gates = [gate_ref[:, k:k + 1] for k in range(TOP_K)]
    for c in range(Y_ROWS):
        cols = slice(c * LANES, (c + 1) * LANES)
        acc = x1_ref[:, cols]
        for k in range(TOP_K):
            acc = acc + gates[k] * ybuf[slot, k, pl.ds(c, COMBINE_TILE, stride=Y_ROWS), :]
        out_ref[:, cols] = acc


def _combine(dest, y_slots, x1, gates):
    n = x1.shape[0]
    steps = n // COMBINE_TILE
    rows = lambda w: pl.BlockSpec((COMBINE_TILE, w), lambda i: (i, 0))
    return pl.pallas_call(
        _combine_kernel,
        grid=(steps,),
        in_specs=[pl.BlockSpec(memory_space=pl.ANY), pl.BlockSpec(memory_space=pl.ANY),
                  rows(D_MODEL), rows(LANES)],
        out_specs=rows(D_MODEL),
        out_shape=jax.ShapeDtypeStruct((n, D_MODEL), F32),
        scratch_shapes=[
            pltpu.SMEM((2 * IDX_WINDOW,), jnp.int32),
            pltpu.VMEM((2, TOP_K, COMBINE_TILE * Y_ROWS, LANES), F32),
            pltpu.SemaphoreType.DMA((2,)),
            pltpu.SemaphoreType.DMA((2,)),
        ],
        compiler_params=pltpu.CompilerParams(
            dimension_semantics=("arbitrary",), vmem_limit_bytes=VMEM_LIMIT),
        name="moe_combine",
    )(dest, y_slots, x1, gates)


def _block_diag_mean(width, group):
    r = jnp.arange(width)[:, None] // group
    c = jnp.arange(width)[None, :] // group
    return jnp.where(r == c, 1.0 / group, 0.0).astype(BF16)


def _per_tile(slots, tile):
    steps = slots.shape[0]
    s = slots.reshape(steps, TOP_K, ROW_TILE // tile, tile).transpose(0, 2, 1, 3)
    return s.reshape(steps * (ROW_TILE // tile), TOP_K * tile)


def _layer(x2, batch, seq, norm1_w, w_in, sgu_norm_w, sgu_w, sgu_b, q_norm_w, k_norm_w,
           out_norm_a_w, out_norm_b_w, w_out, norm2_w, router_w, router_b,
           w_gate_up, b_gate_up, w_down, b_down):
    n = x2.shape[0]
    row = lambda a: a.reshape(1, -1).astype(F32)

    ya, q, k, v = _inproj(
        x2, row(norm1_w), w_in.astype(BF16), row(sgu_norm_w), sgu_w.astype(F32),
        sgu_b.T.astype(F32), row(out_norm_a_w), row(jnp.tile(q_norm_w, SB_HEADS)),
        row(jnp.tile(k_norm_w, SB_HEADS)), _block_diag_mean(D_SB, HEAD_DIM))

    j = jnp.arange(K_TILE)
    later_keys = (j[:, None] > j[None, :]).astype(BF16)
    yb = _attention(q, k, v, later_keys, row(out_norm_b_w), _block_diag_mean(LANES, HEAD_DIM),
                    batch, seq)

    rw = jnp.zeros((D_MODEL, LANES), F32).at[:, :N_EXPERTS].set(router_w.astype(F32))
    rw_hi = rw.astype(BF16)
    rw_lo = (rw - rw_hi.astype(F32)).astype(BF16)
    rb = jnp.full((1, LANES), NEG_BIG, F32).at[0, :N_EXPERTS].set(router_b.astype(F32))
    t = jnp.arange(ROW_TILE)
    earlier_rows = (t[None, :] < t[:, None]).astype(BF16)
    x1, h2p, gates, route_t, counts = _outproj(ya, yb, x2, w_out.astype(BF16), row(norm2_w),
                                               rw_hi, rw_lo, rb, earlier_rows)

    counts = counts[0, :N_EXPERTS]
    padded = (counts + MOE_BLOCK - 1) // MOE_BLOCK * MOE_BLOCK
    pad_end = jnp.cumsum(padded)
    pad_start = pad_end - padded
    n_blocks = -(-(n * TOP_K) // MOE_BLOCK) + N_EXPERTS
    n_used = pad_end[-1] // MOE_BLOCK
    block_start = jnp.arange(n_blocks, dtype=jnp.int32) * MOE_BLOCK
    block_expert = jnp.minimum(
        jnp.sum((pad_end[None, :] <= block_start[:, None]).astype(jnp.int32), axis=1), N_EXPERTS - 1)
    last_expert = jnp.sum(jnp.where(jnp.arange(n_blocks) == n_used - 1, block_expert, 0))
    block_expert = jnp.where(jnp.arange(n_blocks) < n_used, block_expert, last_expert)
    block_expert = block_expert.astype(jnp.int32)
    slots = pad_start[route_t[:, :TOP_K, :]] + route_t[:, TOP_K:, :]
    n_used = n_used.astype(jnp.int32).reshape(1)

    xs = _dispatch(pad_end.astype(jnp.int32), padded.astype(jnp.int32), n_used,
                   (_per_tile(slots, MOVE_TILE) * X_ROWS).astype(jnp.int32), h2p,
                   n_blocks * MOE_BLOCK)
    wg, wu = _split_gate_up(w_gate_up)
    y_slots = _experts(
        block_expert, n_used, xs, wg, wu, w_down.astype(BF16),
        b_gate_up[:, None, 0::2].astype(F32), b_gate_up[:, None, 1::2].astype(F32),
        b_down[:, None, :].astype(F32))
    return _combine((_per_tile(slots, COMBINE_TILE) * Y_ROWS).astype(jnp.int32), y_slots, x1, gates)


def kernel(x, norm1_w, w_in, sgu_norm_w, sgu_w, sgu_b, q_norm_w, k_norm_w, out_norm_a_w,
           out_norm_b_w, w_out, norm2_w, router_w, router_b, w_gate_up, b_gate_up, w_down, b_down):
    batch, seq, d = x.shape
    assert d == D_MODEL and seq % max(ROW_TILE, Q_TILE, K_TILE) == 0 and Q_TILE == K_TILE
    x2 = x.reshape(batch * seq, d)
    for l in range(norm1_w.shape[0]):
        x2 = _layer(x2, batch, seq, norm1_w[l], w_in[l], sgu_norm_w[l], sgu_w[l], sgu_b[l],
                    q_norm_w[l], k_norm_w[l], out_norm_a_w[l], out_norm_b_w[l], w_out[l],
                    norm2_w[l], router_w[l], router_b[l], w_gate_up[l], b_gate_up[l], w_down[l],
                    b_down[l])
    return x2.reshape(batch, seq, d)
```

```python
import jax
import jax.numpy as jnp
from jax import lax
from jax.experimental import pallas as pl
from jax.experimental.pallas import tpu as pltpu

F32 = jnp.float32
BF16 = jnp.bfloat16

D_MODEL = 1024
D_GMLP = 512
GMLP_GROUPS = 4
GROUP_DIM = D_GMLP // GMLP_GROUPS
CHUNK = 128
D_SB = 512
SB_HEADS = 8
HEAD_DIM = D_SB // SB_HEADS
N_EXPERTS = 32
TOP_K = 4
D_FF = 1024
SWIGLU_LIMIT = 7.0
SWIGLU_ALPHA = 1.702
EPS = 1e-5

LANES = 128
LANE_TILES = D_MODEL // LANES
Y_ROWS = LANE_TILES
X_ROWS = LANE_TILES // 2
HEADS_PER_LANE_TILE = LANES // HEAD_DIM
N_HEAD_PAIRS = D_SB // LANES

ROW_TILE = 512
Q_TILE = 256
K_TILE = 256
TICKS_PER_STEP = 4
MOE_BLOCK = 512
MOVE_TILE = 256
COMBINE_TILE = 256
IDX_WINDOW = MOVE_TILE * TOP_K
assert COMBINE_TILE * TOP_K == IDX_WINDOW
SPLIT_ROWS = 256
SPLIT_COLS = 256
NEG_BIG = -1e30
LOG2E = 1.4426950408889634
VMEM_LIMIT = 56 * 1024 * 1024


def _gelu_tanh(x):
    return 0.5 * x * (1.0 + jnp.tanh(0.7978845608028654 * (x + 0.044715 * (x * x * x))))


def _rms(x):
    return lax.rsqrt(jnp.mean(x * x, axis=-1, keepdims=True) + EPS)


def _slot_window(slot):
    return pl.ds(pl.multiple_of(slot * IDX_WINDOW, IDX_WINDOW), IDX_WINDOW)


def _inproj_kernel(x_ref, n1_ref, win_ref, sgn_ref, sgw_ref, sgbt_ref, ona_ref, qn_ref, kn_ref,
                   gm_ref, ya_ref, q_ref, k_ref, v_ref):
    x = x_ref[...]
    h = (x * _rms(x) * n1_ref[...]).astype(BF16)

    z = _gelu_tanh(jnp.dot(h, win_ref[:, :2 * D_GMLP], preferred_element_type=F32))
    row = lax.broadcasted_iota(jnp.int32, (CHUNK, CHUNK), 0)
    col = lax.broadcasted_iota(jnp.int32, (CHUNK, CHUNK), 1)
    tril = col <= row
    for g in range(GMLP_GROUPS):
        w = jnp.where(tril, sgw_ref[g], 0.0).astype(BF16)
        bias = sgbt_ref[:, g:g + 1]
        cu = slice(g * GROUP_DIM, (g + 1) * GROUP_DIM)
        cv = slice(D_GMLP + g * GROUP_DIM, D_GMLP + (g + 1) * GROUP_DIM)
        for c in range(ROW_TILE // CHUNK):
            rows = slice(c * CHUNK, (c + 1) * CHUNK)
            vg = z[rows, cv]
            vn = (vg * _rms(vg) * sgn_ref[:, cu]).astype(BF16)
            mixed = jnp.dot(w, vn, preferred_element_type=F32) + bias
            ya = z[rows, cu] * mixed
            ya_ref[rows, cu] = (ya * _rms(ya) * ona_ref[:, cu]).astype(BF16)

    qkv = jnp.dot(h, win_ref[:, 2 * D_GMLP:], preferred_element_type=F32)
    qp = qkv[:, :D_SB]
    kp = qkv[:, D_SB:2 * D_SB]
    gm = gm_ref[...]
    qms = jnp.dot((qp * qp).astype(BF16), gm, preferred_element_type=F32)
    kms = jnp.dot((kp * kp).astype(BF16), gm, preferred_element_type=F32)
    q_ref[...] = (qp * lax.rsqrt(qms + EPS) * qn_ref[...] * (HEAD_DIM ** -0.5 * LOG2E)).astype(BF16)
    k_ref[...] = (kp * lax.rsqrt(kms + EPS) * kn_ref[...]).astype(BF16)
    v_ref[...] = qkv[:, 2 * D_SB:].astype(BF16)


def _inproj(x2, n1, win, sgn, sgw, sgbt, ona, qn, kn, gm):
    n = x2.shape[0]
    full = lambda shape: pl.BlockSpec(shape, lambda i: (0,) * len(shape))
    row_out = pl.BlockSpec((ROW_TILE, D_SB), lambda i: (i, 0))
    return pl.pallas_call(
        _inproj_kernel,
        grid=(n // ROW_TILE,),
        in_specs=[
            pl.BlockSpec((ROW_TILE, D_MODEL), lambda i: (i, 0)),
            full((1, D_MODEL)),
            full(win.shape),
            full((1, D_GMLP)),
            full(sgw.shape),
            full(sgbt.shape),
            full((1, D_GMLP)),
            full((1, D_SB)),
            full((1, D_SB)),
            full(gm.shape),
        ],
        out_specs=[row_out, row_out, row_out, row_out],
        out_shape=[jax.ShapeDtypeStruct((n, D_SB), BF16)] * 4,
        compiler_params=pltpu.CompilerParams(
            dimension_semantics=("arbitrary",), vmem_limit_bytes=VMEM_LIMIT),
        name="inproj_gmlp",
    )(x2, n1, win, sgn, sgw, sgbt, ona, qn, kn, gm)


def _attn_kernel(q_ref, k_ref, v_ref, u_ref, onb_ref, gp_ref, o_ref,
                 beta_ref, rest_ref, w_ref, acc_ref, later_ref):
    qi = pl.program_id(2)
    q = q_ref[...]
    lane = lax.broadcasted_iota(jnp.int32, (Q_TILE, LANES), 1)
    first = lane < HEAD_DIM
    zero = jnp.zeros_like(q)
    q_heads = (jnp.where(first, q, zero), jnp.where(first, zero, q))
    u = u_ref[...]
    row = lax.broadcasted_iota(jnp.int32, (Q_TILE, K_TILE), 0)
    col = lax.broadcasted_iota(jnp.int32, (Q_TILE, K_TILE), 1)
    strict = col < row

    heads = range(HEADS_PER_LANE_TILE)

    def keys_of(i):
        return pl.ds(pl.multiple_of(jnp.maximum(qi - i, 0) * K_TILE, K_TILE), K_TILE)

    def stage_scores(i, p, mask=None, valid=None):
        kj = k_ref[keys_of(i), :]
        for hd in heads:
            z = lax.dot_general(q_heads[hd], kj, (((1,), (1,)), ((), ())),
                                preferred_element_type=F32)
            log_beta = jnp.minimum(z, 0.0) - jnp.log(1.0 + jnp.exp2(-jnp.abs(z))) * LOG2E
            log_rest = log_beta - z
            if mask is not None:
                log_rest = jnp.where(mask, log_rest, 0.0)
            if valid is not None:
                log_rest = jnp.where(valid, log_rest, 0.0)
                log_beta = jnp.where(valid, log_beta, NEG_BIG)
            later = later_ref[hd]
            beta_ref[p, hd] = log_beta + later
            rest_ref[p, hd] = log_rest.astype(BF16)
            later_ref[hd] = later + jnp.sum(log_rest, axis=-1, keepdims=True)

    def stage_weights(p, mask=None):
        for hd in heads:
            within = jnp.dot(rest_ref[p, hd], u, preferred_element_type=F32)
            a = jnp.exp2(beta_ref[p, hd] + within)
            if mask is not None:
                a = jnp.where(mask, a, 0.0)
            w_ref[p, hd] = a.astype(BF16)

    def stage_values(i, p):
        vj = v_ref[keys_of(i), :]
        for hd in heads:
            acc_ref[hd] += jnp.dot(w_ref[p, hd], vj, preferred_element_type=F32)

    def tick(t, p):
        stage_values(t - 2, p)
        stage_weights(1 - p)
        stage_scores(t, p)

    acc_ref[...] = jnp.zeros_like(acc_ref)
    later_ref[...] = jnp.zeros_like(later_ref)
    stage_scores(0, 0, mask=strict)
    stage_weights(0, mask=strict)
    stage_scores(1, 1, valid=qi >= 1)
    def run_ticks(first_tick, n_steps, ticks_per_step):
        def body(s, c):
            for d in range(ticks_per_step):
                tick(first_tick + ticks_per_step * s + d, d % 2)
            return c
        lax.fori_loop(0, n_steps, body, 0)
        return first_tick + ticks_per_step * n_steps

    ticks_left = jnp.maximum(qi - 1, 0)
    next_tick = 2
    for ticks_per_step in (TICKS_PER_STEP, 2, 1):
        n_steps = ticks_left // ticks_per_step
        next_tick = run_ticks(next_tick, n_steps, ticks_per_step)
        ticks_left = ticks_left - n_steps * ticks_per_step
    drain = jnp.maximum(qi + 1, 2)
    stage_values(drain - 2, drain % 2)
    stage_weights(1 - drain % 2)
    stage_values(drain - 1, 1 - drain % 2)
    acc = acc_ref[...]
    o = jnp.where(first, acc[0], acc[1])
    ms = jnp.dot((o * o).astype(BF16), gp_ref[...], preferred_element_type=F32)
    o_ref[...] = (o * lax.rsqrt(ms + EPS) * onb_ref[...]).astype(BF16)


def _attention(q, k, v, u, onb, gp, batch, seq):
    n = q.shape[0]
    nq = seq // Q_TILE
    q_spec = pl.BlockSpec((Q_TILE, LANES), lambda b, p, i: (b * nq + i, p))
    kv_spec = pl.BlockSpec((seq, LANES), lambda b, p, i: (b, p))
    return pl.pallas_call(
        _attn_kernel,
        grid=(batch, N_HEAD_PAIRS, nq),
        in_specs=[
            q_spec, kv_spec, kv_spec,
            pl.BlockSpec(u.shape, lambda b, p, i: (0, 0)),
            pl.BlockSpec((1, LANES), lambda b, p, i: (0, p)),
            pl.BlockSpec(gp.shape, lambda b, p, i: (0, 0)),
        ],
        out_specs=q_spec,
        out_shape=jax.ShapeDtypeStruct((n, D_SB), BF16),
        scratch_shapes=[
            pltpu.VMEM((2, HEADS_PER_LANE_TILE, Q_TILE, K_TILE), F32),
            pltpu.VMEM((2, HEADS_PER_LANE_TILE, Q_TILE, K_TILE), BF16),
            pltpu.VMEM((2, HEADS_PER_LANE_TILE, Q_TILE, K_TILE), BF16),
            pltpu.VMEM((HEADS_PER_LANE_TILE, Q_TILE, LANES), F32),
            pltpu.VMEM((HEADS_PER_LANE_TILE, Q_TILE, 1), F32),
        ],
        compiler_params=pltpu.CompilerParams(
            dimension_semantics=("arbitrary",) * 3, vmem_limit_bytes=VMEM_LIMIT),
        name="stickbreak_attn",
    )(q, k, v, u, onb, gp)


def _outproj_kernel(ya_ref, yb_ref, x_ref, wout_ref, n2_ref, rwh_ref, rwl_ref, rb_ref, ltri_ref,
                    x1_ref, h2p_ref, gate_ref, route_t_ref, cnt_ref, seen_ref, tok_ref):
    @pl.when(pl.program_id(0) == 0)
    def _():
        seen_ref[...] = jnp.zeros_like(seen_ref)

    x1 = (x_ref[...]
          + jnp.dot(ya_ref[...], wout_ref[:D_GMLP, :], preferred_element_type=F32)
          + jnp.dot(yb_ref[...], wout_ref[D_GMLP:, :], preferred_element_type=F32))
    x1_ref[...] = x1
    h2 = x1 * _rms(x1) * n2_ref[...]
    h_hi = h2.astype(BF16)

    for c in range(Y_ROWS):
        tok_ref[pl.ds(c, ROW_TILE, stride=Y_ROWS), :] = h2[:, c * LANES:(c + 1) * LANES]
    h2p_ref[...] = pltpu.bitcast(tok_ref[...].astype(BF16), jnp.uint32)

    h_lo = (h2 - h_hi.astype(F32)).astype(BF16)
    rwh = rwh_ref[...]
    logits = (jnp.dot(h_hi, rwh, preferred_element_type=F32)
              + jnp.dot(h_lo, rwh, preferred_element_type=F32)
              + jnp.dot(h_hi, rwl_ref[...], preferred_element_type=F32)
              + rb_ref[...])

    lane = lax.broadcasted_iota(jnp.int32, (ROW_TILE, LANES), 1)
    work = logits
    onehot = jnp.zeros((ROW_TILE, LANES), F32)
    top_logit, top_idx = [], []
    for _ in range(TOP_K):
        m = jnp.max(work, axis=-1, keepdims=True)
        idx = jnp.min(jnp.where(work == m, lane, LANES), axis=-1, keepdims=True)
        sel = lane == idx
        top_logit.append(m)
        top_idx.append(idx)
        onehot = jnp.where(sel, 1.0, onehot)
        work = jnp.where(sel, -jnp.inf, work)
    ex = [jnp.exp(m - top_logit[0]) for m in top_logit]
    denom = ex[0] + ex[1] + ex[2] + ex[3]

    seen = seen_ref[...]
    before = jnp.dot(ltri_ref[...], onehot.astype(BF16), preferred_element_type=F32) + seen
    seen_new = seen + jnp.sum(onehot, axis=0, keepdims=True)
    seen_ref[...] = seen_new
    cnt_ref[...] = jnp.broadcast_to(seen_new, cnt_ref.shape).astype(jnp.int32)

    route = jnp.zeros((ROW_TILE, LANES), jnp.int32)
    gates = jnp.zeros((ROW_TILE, LANES), F32)
    for k in range(TOP_K):
        rank = jnp.sum(jnp.where(lane == top_idx[k], before, 0.0), axis=-1, keepdims=True)
        route = jnp.where(lane == k, top_idx[k], route)
        route = jnp.where(lane == TOP_K + k, rank.astype(jnp.int32), route)
        gates = jnp.where(lane == k, ex[k] / denom, gates)
    gate_ref[...] = gates
    route_t_ref[0] = route.T[:2 * TOP_K, :]


def _outproj(ya, yb, x2, wout, n2, rwh, rwl, rb, ltri):
    n = x2.shape[0]
    steps = n // ROW_TILE
    full = lambda shape: pl.BlockSpec(shape, lambda i: (0,) * len(shape))
    rows = lambda w: pl.BlockSpec((ROW_TILE, w), lambda i: (i, 0))
    return pl.pallas_call(
        _outproj_kernel,
        grid=(steps,),
        in_specs=[rows(D_GMLP), rows(D_SB), rows(D_MODEL), full(wout.shape), full((1, D_MODEL)),
                  full(rwh.shape), full(rwl.shape), full((1, LANES)), full(ltri.shape)],
        out_specs=[rows(D_MODEL),
                   pl.BlockSpec((ROW_TILE * X_ROWS, LANES), lambda i: (i, 0)),
                   rows(LANES),
                   pl.BlockSpec((1, 2 * TOP_K, ROW_TILE), lambda i: (i, 0, 0)),
                   full((8, LANES))],
        out_shape=[jax.ShapeDtypeStruct((n, D_MODEL), F32),
                   jax.ShapeDtypeStruct((n * X_ROWS, LANES), jnp.uint32),
                   jax.ShapeDtypeStruct((n, LANES), F32),
                   jax.ShapeDtypeStruct((steps, 2 * TOP_K, ROW_TILE), jnp.int32),
                   jax.ShapeDtypeStruct((8, LANES), jnp.int32)],
        scratch_shapes=[pltpu.VMEM((1, LANES), F32), pltpu.VMEM((ROW_TILE * Y_ROWS, LANES), F32)],
        compiler_params=pltpu.CompilerParams(
            dimension_semantics=("arbitrary",), vmem_limit_bytes=VMEM_LIMIT),
        name="outproj_router",
    )(ya, yb, x2, wout, n2, rwh, rwl, rb, ltri)


def _split_kernel(w_ref, perm_ref, wg_ref, wu_ref):
    perm = perm_ref[...]
    half = SPLIT_COLS // 2
    for c in range(2 * D_FF // SPLIT_COLS):
        chunk = w_ref[0, :, c * SPLIT_COLS:(c + 1) * SPLIT_COLS].astype(BF16)
        r = jnp.dot(chunk, perm, preferred_element_type=F32)
        wg_ref[0, :, c * half:(c + 1) * half] = r[:, :half].astype(BF16)
        wu_ref[0, :, c * half:(c + 1) * half] = r[:, half:].astype(BF16)


def _split_gate_up(w_gate_up):
    e, d, f2 = w_gate_up.shape
    j = jnp.arange(SPLIT_COLS)
    target = jnp.where(j % 2 == 0, j // 2, SPLIT_COLS // 2 + j // 2)
    perm = (target[:, None] == j[None, :]).astype(BF16)
    out_spec = pl.BlockSpec((1, SPLIT_ROWS, f2 // 2), lambda i, r: (i, r, 0))
    return pl.pallas_call(
        _split_kernel,
        grid=(e, d // SPLIT_ROWS),
        in_specs=[pl.BlockSpec((1, SPLIT_ROWS, f2), lambda i, r: (i, r, 0)),
                  pl.BlockSpec(perm.shape, lambda i, r: (0, 0))],
        out_specs=[out_spec, out_spec],
        out_shape=[jax.ShapeDtypeStruct((e, d, f2 // 2), BF16)] * 2,
        compiler_params=pltpu.CompilerParams(
            dimension_semantics=("arbitrary",) * 2, vmem_limit_bytes=VMEM_LIMIT),
        name="split_gate_up",
    )(w_gate_up, perm)


def _dispatch_kernel(pend_ref, padded_ref, nused_ref, dest_hbm, h2p_ref, xs_hbm,
                     dest_smem, zero_buf, idx_sem, zero_sem, row_sem):
    i = pl.program_id(0)
    steps = pl.num_programs(0)
    block_rows = MOE_BLOCK * X_ROWS
    n_blocks = xs_hbm.shape[0] // block_rows

    def idx_copy(step, slot):
        return pltpu.make_async_copy(dest_hbm.at[step], dest_smem.at[_slot_window(slot)],
                                     idx_sem.at[slot])

    def zero_copy(start):
        return pltpu.make_async_copy(zero_buf, xs_hbm.at[pl.ds(start, block_rows)], zero_sem)

    @pl.when(i == 0)
    def _():
        idx_copy(0, 0).start()
        zero_buf[...] = jnp.zeros_like(zero_buf)
        for e in range(N_EXPERTS):
            @pl.when(padded_ref[e] > 0)
            def _():
                last_block = pend_ref[e] // MOE_BLOCK - 1
                zero_copy(pl.multiple_of(last_block * block_rows, block_rows)).start()
        lax.fori_loop(
            nused_ref[0], n_blocks,
            lambda b, c: (zero_copy(pl.multiple_of(b * block_rows, block_rows)).start(), c)[1], 0)
        for e in range(N_EXPERTS):
            @pl.when(padded_ref[e] > 0)
            def _():
                zero_copy(0).wait()
        lax.fori_loop(nused_ref[0], n_blocks, lambda b, c: (zero_copy(0).wait(), c)[1], 0)

    slot = i % 2
    idx_copy(i, slot).wait()

    @pl.when(i + 1 < steps)
    def _():
        idx_copy(i + 1, 1 - slot).start()

    def token(r, c):
        src = h2p_ref.at[pl.ds(pl.multiple_of(r * X_ROWS, X_ROWS), X_ROWS)]
        for k in range(TOP_K):
            d = pl.multiple_of(dest_smem[slot * IDX_WINDOW + k * MOVE_TILE + r], X_ROWS)
            pltpu.make_async_copy(src, xs_hbm.at[pl.ds(d, X_ROWS)], row_sem).start(priority=k % 2)
        return c

    lax.fori_loop(0, MOVE_TILE, token, 0, unroll=8)
    for k in range(TOP_K):
        pltpu.make_async_copy(h2p_ref, xs_hbm.at[pl.ds(0, MOVE_TILE * X_ROWS)], row_sem).wait()


def _dispatch(pad_end, padded, n_used, dest, h2p, n_slots):
    steps = h2p.shape[0] // (MOVE_TILE * X_ROWS)
    return pl.pallas_call(
        _dispatch_kernel,
        grid_spec=pltpu.PrefetchScalarGridSpec(
            num_scalar_prefetch=3,
            grid=(steps,),
            in_specs=[pl.BlockSpec(memory_space=pl.ANY),
                      pl.BlockSpec((MOVE_TILE * X_ROWS, LANES), lambda i, pe, pd, nu: (i, 0))],
            out_specs=pl.BlockSpec(memory_space=pl.ANY),
            scratch_shapes=[
                pltpu.SMEM((2 * IDX_WINDOW,), jnp.int32),
                pltpu.VMEM((MOE_BLOCK * X_ROWS, LANES), jnp.uint32),
                pltpu.SemaphoreType.DMA((2,)),
                pltpu.SemaphoreType.DMA,
                pltpu.SemaphoreType.DMA,
            ]),
        out_shape=jax.ShapeDtypeStruct((n_slots * X_ROWS, LANES), jnp.uint32),
        compiler_params=pltpu.CompilerParams(
            dimension_semantics=("arbitrary",), vmem_limit_bytes=VMEM_LIMIT),
        name="moe_dispatch",
    )(pad_end, padded, n_used, dest, h2p)


def _expert_kernel(be_ref, nused_ref, xs_ref, wg_ref, wu_ref, wd_ref, bg_ref, bu_ref, bd_ref,
                   y_ref, tok_ref):
    b = pl.program_id(0)

    @pl.when(b < nused_ref[0])
    def _():
        tok_ref[...] = pltpu.bitcast(xs_ref[...], BF16).astype(F32)
        x = jnp.concatenate(
            [tok_ref[pl.ds(c, MOE_BLOCK, stride=Y_ROWS), :].astype(BF16) for c in range(Y_ROWS)],
            axis=-1)
        gate = jnp.dot(x, wg_ref[0], preferred_element_type=F32) + bg_ref[0]
        up = jnp.dot(x, wu_ref[0], preferred_element_type=F32) + bu_ref[0]
        gate = jnp.minimum(gate, SWIGLU_LIMIT)
        up = jnp.clip(up, -SWIGLU_LIMIT, SWIGLU_LIMIT)
        glu = gate / (1.0 + jnp.exp(-SWIGLU_ALPHA * gate))
        hidden = ((up + 1.0) * glu).astype(BF16)
        y = jnp.dot(hidden, wd_ref[0], preferred_element_type=F32) + bd_ref[0]
        for c in range(Y_ROWS):
            y_ref[pl.ds(c, MOE_BLOCK, stride=Y_ROWS), :] = y[:, c * LANES:(c + 1) * LANES]

    @pl.when(b >= nused_ref[0])
    def _():
        y_ref[...] = jnp.zeros_like(y_ref)


def _experts(block_expert, n_used, xs, wg, wu, wd, bg, bu, bd):
    n_blocks = xs.shape[0] // (MOE_BLOCK * X_ROWS)
    w_spec = lambda: pl.BlockSpec((1, D_MODEL, D_FF), lambda b, be, nu: (be[b], 0, 0))
    b_spec = lambda: pl.BlockSpec((1, 1, D_FF), lambda b, be, nu: (be[b], 0, 0))
    return pl.pallas_call(
        _expert_kernel,
        grid_spec=pltpu.PrefetchScalarGridSpec(
            num_scalar_prefetch=2,
            grid=(n_blocks,),
            in_specs=[
                pl.BlockSpec((MOE_BLOCK * X_ROWS, LANES),
                             lambda b, be, nu: (jnp.minimum(b, jnp.maximum(nu[0] - 1, 0)), 0)),
                w_spec(), w_spec(), w_spec(), b_spec(), b_spec(), b_spec(),
            ],
            out_specs=pl.BlockSpec((MOE_BLOCK * Y_ROWS, LANES), lambda b, be, nu: (b, 0)),
            scratch_shapes=[pltpu.VMEM((MOE_BLOCK * Y_ROWS, LANES), F32)]),
        out_shape=jax.ShapeDtypeStruct((n_blocks * MOE_BLOCK * Y_ROWS, LANES), F32),
        compiler_params=pltpu.CompilerParams(
            dimension_semantics=("arbitrary",), vmem_limit_bytes=VMEM_LIMIT),
        name="moe_experts",
    )(block_expert, n_used, xs, wg, wu, wd, bg, bu, bd)


def _combine_kernel(dest_hbm, y_hbm, x1_ref, gate_ref, out_ref, dest_smem, ybuf, idx_sem, row_sem):
    i = pl.program_id(0)
    steps = pl.num_programs(0)

    def idx_copy(step, slot):
        return pltpu.make_async_copy(dest_hbm.at[step], dest_smem.at[_slot_window(slot)],
                                     idx_sem.at[slot])

    def start_rows(slot):
        def token(r, c):
            for k in range(TOP_K):
                d = pl.multiple_of(dest_smem[slot * IDX_WINDOW + k * COMBINE_TILE + r], Y_ROWS)
                dst = ybuf.at[slot, k, pl.ds(pl.multiple_of(r * Y_ROWS, Y_ROWS), Y_ROWS)]
                pltpu.make_async_copy(y_hbm.at[pl.ds(d, Y_ROWS)], dst,
                                      row_sem.at[slot]).start(priority=k % 2)
            return c
        lax.fori_loop(0, COMBINE_TILE, token, 0, unroll=8)

    slot = i % 2

    @pl.when(i == 0)
    def _():
        idx_copy(0, 0).start()
        idx_copy(0, 0).wait()
        start_rows(0)

        @pl.when(steps > 1)
        def _():
            idx_copy(1, 1).start()

    @pl.when(i + 1 < steps)
    def _():
        idx_copy(i + 1, 1 - slot).wait()
        start_rows(1 - slot)

        @pl.when(i + 2 < steps)
        def _():
            idx_copy(i + 2, slot).start()

    for k in range(TOP_K):
        pltpu.make_async_copy(y_hbm.at[pl.ds(0, COMBINE_TILE * Y_ROWS)], ybuf.at[slot, k],
                              row_sem.at[slot]).wait()
    gates = [gate_ref[:, k:k + 1] for k in range(TOP_K)]
    for c in range(Y_ROWS):
        cols = slice(c * LANES, (c + 1) * LANES)
        acc = x1_ref[:, cols]
        for k in range(TOP_K):
            acc = acc + gates[k] * ybuf[slot, k, pl.ds(c, COMBINE_TILE, stride=Y_ROWS), :]
        out_ref[:, cols] = acc


def _combine(dest, y_slots, x1, gates):
    n = x1.shape[0]
    steps = n // COMBINE_TILE
    rows = lambda w: pl.BlockSpec((COMBINE_TILE, w), lambda i: (i, 0))
    return pl.pallas_call(
        _combine_kernel,
        grid=(steps,),
        in_specs=[pl.BlockSpec(memory_space=pl.ANY), pl.BlockSpec(memory_space=pl.ANY),
                  rows(D_MODEL), rows(LANES)],
        out_specs=rows(D_MODEL),
        out_shape=jax.ShapeDtypeStruct((n, D_MODEL), F32),
        scratch_shapes=[
            pltpu.SMEM((2 * IDX_WINDOW,), jnp.int32),
            pltpu.VMEM((2, TOP_K, COMBINE_TILE * Y_ROWS, LANES), F32),
            pltpu.SemaphoreType.DMA((2,)),
            pltpu.SemaphoreType.DMA((2,)),
        ],
        compiler_params=pltpu.CompilerParams(
            dimension_semantics=("arbitrary",), vmem_limit_bytes=VMEM_LIMIT),
        name="moe_combine",
    )(dest, y_slots, x1, gates)


def _block_diag_mean(width, group):
    r = jnp.arange(width)[:, None] // group
    c = jnp.arange(width)[None, :] // group
    return jnp.where(r == c, 1.0 / group, 0.0).astype(BF16)


def _per_tile(slots, tile):
    steps = slots.shape[0]
    s = slots.reshape(steps, TOP_K, ROW_TILE // tile, tile).transpose(0, 2, 1, 3)
    return s.reshape(steps * (ROW_TILE // tile), TOP_K * tile)


def _layer(x2, batch, seq, norm1_w, w_in, sgu_norm_w, sgu_w, sgu_b, q_norm_w, k_norm_w,
           out_norm_a_w, out_norm_b_w, w_out, norm2_w, router_w, router_b,
           w_gate_up, b_gate_up, w_down, b_down):
    n = x2.shape[0]
    row = lambda a: a.reshape(1, -1).astype(F32)

    ya, q, k, v = _inproj(
        x2, row(norm1_w), w_in.astype(BF16), row(sgu_norm_w), sgu_w.astype(F32),
        sgu_b.T.astype(F32), row(out_norm_a_w), row(jnp.tile(q_norm_w, SB_HEADS)),
        row(jnp.tile(k_norm_w, SB_HEADS)), _block_diag_mean(D_SB, HEAD_DIM))

    j = jnp.arange(K_TILE)
    later_keys = (j[:, None] > j[None, :]).astype(BF16)
    yb = _attention(q, k, v, later_keys, row(out_norm_b_w), _block_diag_mean(LANES, HEAD_DIM),
                    batch, seq)

    rw = jnp.zeros((D_MODEL, LANES), F32).at[:, :N_EXPERTS].set(router_w.astype(F32))
    rw_hi = rw.astype(BF16)
    rw_lo = (rw - rw_hi.astype(F32)).astype(BF16)
    rb = jnp.full((1, LANES), NEG_BIG, F32).at[0, :N_EXPERTS].set(router_b.astype(F32))
    t = jnp.arange(ROW_TILE)
    earlier_rows = (t[None, :] < t[:, None]).astype(BF16)
    x1, h2p, gates, route_t, counts = _outproj(ya, yb, x2, w_out.astype(BF16), row(norm2_w),
                                               rw_hi, rw_lo, rb, earlier_rows)

    counts = counts[0, :N_EXPERTS]
    padded = (counts + MOE_BLOCK - 1) // MOE_BLOCK * MOE_BLOCK
    pad_end = jnp.cumsum(padded)
    pad_start = pad_end - padded
    n_blocks = -(-(n * TOP_K) // MOE_BLOCK) + N_EXPERTS
    n_used = pad_end[-1] // MOE_BLOCK
    block_start = jnp.arange(n_blocks, dtype=jnp.int32) * MOE_BLOCK
    block_expert = jnp.minimum(
        jnp.sum((pad_end[None, :] <= block_start[:, None]).astype(jnp.int32), axis=1), N_EXPERTS - 1)
    last_expert = jnp.sum(jnp.where(jnp.arange(n_blocks) == n_used - 1, block_expert, 0))
    block_expert = jnp.where(jnp.arange(n_blocks) < n_used, block_expert, last_expert)
    block_expert = block_expert.astype(jnp.int32)
    expert_t = route_t[:, :TOP_K, :]
    slots = route_t[:, TOP_K:, :]
    for e in range(N_EXPERTS):
        slots = slots + jnp.where(expert_t == e, pad_start[e], 0)
    n_used = n_used.astype(jnp.int32).reshape(1)

    xs = _dispatch(pad_end.astype(jnp.int32), padded.astype(jnp.int32), n_used,
                   (_per_tile(slots, MOVE_TILE) * X_ROWS).astype(jnp.int32), h2p,
                   n_blocks * MOE_BLOCK)
    wg, wu = _split_gate_up(w_gate_up)
    y_slots = _experts(
        block_expert, n_used, xs, wg, wu, w_down.astype(BF16),
        b_gate_up[:, None, 0::2].astype(F32), b_gate_up[:, None, 1::2].astype(F32),
        b_down[:, None, :].astype(F32))
    return _combine((_per_tile(slots, COMBINE_TILE) * Y_ROWS).astype(jnp.int32), y_slots, x1, gates)


def kernel(x, norm1_w, w_in, sgu_norm_w, sgu_w, sgu_b, q_norm_w, k_norm_w, out_norm_a_w,
           out_norm_b_w, w_out, norm2_w, router_w, router_b, w_gate_up, b_gate_up, w_down, b_down):
    batch, seq, d = x.shape
    assert d == D_MODEL and seq % max(ROW_TILE, Q_TILE, K_TILE) == 0 and Q_TILE == K_TILE
    x2 = x.reshape(batch * seq, d)
    for l in range(norm1_w.shape[0]):
        x2 = _layer(x2, batch, seq, norm1_w[l], w_in[l], sgu_norm_w[l], sgu_w[l], sgu_b[l],
                    q_norm_w[l], k_norm_w[l], out_norm_a_w[l], out_norm_b_w[l], w_out[l],
                    norm2_w[l], router_w[l], router_b[l], w_gate_up[l], b_gate_up[l], w_down[l],
                    b_down[l])
    return x2.reshape(batch, seq, d)
```
